```python
import math
import jax
import jax.numpy as jnp
from jax import lax
import numpy as np

D_MODEL = 1024
BATCH = 8
SEQ = 2048
DEPTH = 4
DEC_BATCH = 128
DEC_SEQ = 4
PAST_LEN = 2048
PAGE_SIZE = 128

N_MEM = 256
CONV_CH = D_MODEL // 4
CONV_W = 3
DIFF_HEADS = 4
DIFF_DH = 64
DIFF_WIDTH = DIFF_HEADS * 2 * DIFF_DH
SSM_CH = D_MODEL // 4
SSM_GROUP = 16
SSM_GROUPS = SSM_CH // SSM_GROUP
SSM_P = 64
MIX_WIDTH = CONV_CH + DIFF_WIDTH + SSM_CH
IN_COLS = 3 * CONV_CH + 3 * DIFF_WIDTH + SSM_CH
SPLIT_POINTS = (CONV_CH, 2 * CONV_CH, 3 * CONV_CH, 3 * CONV_CH + DIFF_WIDTH, 3 * CONV_CH + 2 * DIFF_WIDTH, 3 * CONV_CH + 3 * DIFF_WIDTH)
X_HEADS = 4
X_DH = D_MODEL // X_HEADS
D_FF = 4 * D_MODEL
ROPE_THETA = 500000.0
ROPE_DIM = DIFF_DH // 4
Q_BLOCK = 128
EPS = 1e-6

kernel_name = 'hybrid_conv_diffattn_s5_decode_step'


def rmsnorm(x, g):
    xf = x.astype(jnp.float32)
    y = xf * lax.rsqrt(jnp.mean(xf * xf, axis=-1, keepdims=True) + EPS)
    return (y * g.astype(jnp.float32)).astype(x.dtype)


def rope_partial(x, pos):
    half = ROPE_DIM // 2
    inv_freq = ROPE_THETA ** (-jnp.arange(half, dtype=jnp.float32) / half)
    ang = pos.astype(jnp.float32)[:, None] * inv_freq[None, :]
    cos = jnp.cos(ang)[None, :, None, None, :]
    sin = jnp.sin(ang)[None, :, None, None, :]
    x1 = x[..., :half].astype(jnp.float32)
    x2 = x[..., half:ROPE_DIM].astype(jnp.float32)
    rot = jnp.concatenate([x1 * cos - x2 * sin, x2 * cos + x1 * sin], axis=-1).astype(x.dtype)
    return jnp.concatenate([rot, x[..., ROPE_DIM:]], axis=-1)


def short_conv(b_gate, c_gate, x_in, conv_w, conv_state):
    L = x_in.shape[1]
    u = c_gate * x_in
    ext = jnp.concatenate([conv_state.astype(u.dtype), u], axis=1)
    y = sum(conv_w[j] * ext[:, j:j + L] for j in range(CONV_W))
    return b_gate * y, ext[:, L:]


def diff_probs(s, lam):
    p = jax.nn.softmax(s, axis=-1)
    return p[:, :, 0] - lam * p[:, :, 1]


def diff_attn_prompt(q, k, v, lam):
    Bsz, L = q.shape[0], q.shape[1]
    nb = L // Q_BLOCK
    scale = 1.0 / math.sqrt(DIFF_DH)
    qb = jnp.moveaxis(q.reshape(Bsz, nb, Q_BLOCK, DIFF_HEADS, 2, DIFF_DH), 1, 0)
    kpos = jnp.arange(L, dtype=jnp.int32)

    def block(args):
        qi, i = args
        s = jnp.einsum('bqhcd,bkhcd->bhcqk', qi, k).astype(jnp.float32) * scale
        qpos = i * Q_BLOCK + jnp.arange(Q_BLOCK, dtype=jnp.int32)
        mask = kpos[None, :] <= qpos[:, None]
        s = jnp.where(mask, s, -jnp.inf)
        a = diff_probs(s, lam).astype(v.dtype)
        return jnp.einsum('bhqk,bkhe->bqhe', a, v)

    o = lax.map(block, (qb, jnp.arange(nb, dtype=jnp.int32)))
    return jnp.moveaxis(o, 0, 1).reshape(Bsz, L, DIFF_HEADS, 2 * DIFF_DH)


def diff_attn_sample(q, k_new, v_new, k_past, v_past, lam):
    Lq = q.shape[1]
    P = k_past.shape[1]
    scale = 1.0 / math.sqrt(DIFF_DH)
    s_past = jnp.einsum('bqhcd,bkhcd->bhcqk', q, k_past.astype(q.dtype)).astype(jnp.float32) * scale
    s_new = jnp.einsum('bqhcd,bkhcd->bhcqk', q, k_new).astype(jnp.float32) * scale
    causal = jnp.tril(jnp.ones((Lq, Lq), dtype=bool))
    s_new = jnp.where(causal, s_new, -jnp.inf)
    a = diff_probs(jnp.concatenate([s_past, s_new], axis=-1), lam).astype(v_new.dtype)
    return (jnp.einsum('bhqk,bkhe->bqhe', a[..., :P], v_past.astype(v_new.dtype))
            + jnp.einsum('bhqk,bkhe->bqhe', a[..., P:], v_new))


def _ssm_combine(e1, e2):
    a1r, a1i, b1r, b1i = e1
    a2r, a2i, b2r, b2i = e2
    return (a2r * a1r - a2i * a1i,
            a2r * a1i + a2i * a1r,
            a2r * b1r - a2i * b1i + b2r,
            a2r * b1i + a2i * b1r + b2i)


def s5(u, h0_re, h0_im, a_re, a_im, log_dt, b_re, b_im, c_re, c_im, d_skip, w_glu):
    f32 = jnp.float32
    Bsz, L, _ = u.shape
    uf = u.astype(f32)
    ug = uf.reshape(Bsz, L, SSM_GROUPS, SSM_GROUP)
    a_re = a_re.astype(f32)
    a_im = a_im.astype(f32)
    dt = jnp.exp(log_dt.astype(f32))[:, None]
    mag = jnp.exp(a_re * dt)
    ab_re = mag * jnp.cos(a_im * dt)
    ab_im = mag * jnp.sin(a_im * dt)
    den = a_re * a_re + a_im * a_im
    nr = ab_re - 1.0
    g_re = (nr * a_re + ab_im * a_im) / den
    g_im = (ab_im * a_re - nr * a_im) / den
    bu_re = jnp.einsum('blgh,gph->blgp', ug, b_re.astype(f32))
    bu_im = jnp.einsum('blgh,gph->blgp', ug, b_im.astype(f32))
    x_re = g_re * bu_re - g_im * bu_im
    x_im = g_re * bu_im + g_im * bu_re
    h0r = h0_re.astype(f32)
    h0i = h0_im.astype(f32)
    x_re = x_re.at[:, 0].add(ab_re * h0r - ab_im * h0i)
    x_im = x_im.at[:, 0].add(ab_re * h0i + ab_im * h0r)
    a_r = jnp.broadcast_to(ab_re, x_re.shape)
    a_i = jnp.broadcast_to(ab_im, x_re.shape)
    _, _, h_re, h_im = lax.associative_scan(_ssm_combine, (a_r, a_i, x_re, x_im), axis=1)
    y = (jnp.einsum('blgp,ghp->blgh', h_re, c_re.astype(f32))
         - jnp.einsum('blgp,ghp->blgh', h_im, c_im.astype(f32)))
    y = y.reshape(Bsz, L, SSM_CH) + d_skip.astype(f32) * uf
    y = jax.nn.gelu(y)
    y = y * jax.nn.sigmoid(y @ w_glu.astype(f32))
    return y.astype(u.dtype), h_re[:, -1], h_im[:, -1]


def mem_kv(mem, g, w_k, w_v):
    Bsz, M, _ = mem.shape
    m = rmsnorm(mem, g)
    return (m @ w_k).reshape(Bsz, M, X_HEADS, X_DH), (m @ w_v).reshape(Bsz, M, X_HEADS, X_DH)


def cross_attn(h, mem_k, mem_v, w_q, w_o):
    Bsz, L, _ = h.shape
    q = (h @ w_q).reshape(Bsz, L, X_HEADS, X_DH)
    s = jnp.einsum('bqhd,bkhd->bhqk', q, mem_k.astype(q.dtype)).astype(jnp.float32) / math.sqrt(X_DH)
    p = jax.nn.softmax(s, axis=-1).astype(h.dtype)
    o = jnp.einsum('bhqk,bkhd->bqhd', p, mem_v.astype(h.dtype)).reshape(Bsz, L, D_MODEL)
    return o @ w_o


def hybrid_layer(l, W, x, pos, conv_state, h_re, h_im, mem_k, mem_v, k_past, v_past):
    Bsz, L, _ = x.shape
    lam_init = 0.8 - 0.6 * math.exp(-0.3 * l)
    h = rmsnorm(x, W['norm_mix'][l])
    z = h @ W['w_in'][l]
    b_g, c_g, x_in, q, k, v, u = jnp.split(z, SPLIT_POINTS, axis=-1)
    y_a, new_conv = short_conv(b_g, c_g, x_in, W['conv_w'][l], conv_state)
    q = rope_partial(q.reshape(Bsz, L, DIFF_HEADS, 2, DIFF_DH), pos)
    k = rope_partial(k.reshape(Bsz, L, DIFF_HEADS, 2, DIFF_DH), pos)
    v = v.reshape(Bsz, L, DIFF_HEADS, 2 * DIFF_DH)
    f32 = jnp.float32
    lam = (jnp.exp(jnp.sum(W['lambda_q1'][l].astype(f32) * W['lambda_k1'][l].astype(f32)))
           - jnp.exp(jnp.sum(W['lambda_q2'][l].astype(f32) * W['lambda_k2'][l].astype(f32)))
           + lam_init)
    if k_past is None:
        o = diff_attn_prompt(q, k, v, lam)
    else:
        o = diff_attn_sample(q, k, v, k_past, v_past, lam)
    o = rmsnorm(o, W['subln_w'][l]) * (1.0 - lam_init)
    y_b = o.reshape(Bsz, L, DIFF_WIDTH)
    y_c, nh_re, nh_im = s5(u, h_re, h_im, W['ssm_a_re'][l], W['ssm_a_im'][l], W['ssm_log_dt'][l],
                           W['ssm_b_re'][l], W['ssm_b_im'][l], W['ssm_c_re'][l], W['ssm_c_im'][l],
                           W['ssm_d'][l], W['ssm_w_glu'][l])
    x = x + jnp.concatenate([y_a, y_b, y_c], axis=-1) @ W['w_o'][l]
    x = x + cross_attn(rmsnorm(x, W['norm_x'][l]), mem_k, mem_v, W['w_xq'][l], W['w_xo'][l])
    hm = rmsnorm(x, W['norm_mlp'][l])
    x = x + jnp.square(jax.nn.relu(hm @ W['w_up'][l])) @ W['w_down'][l]
    return x, k, v, new_conv, nh_re, nh_im


def setup_inputs(seed: int = 0) -> dict:
    key = jax.random.key(seed)
    ks = jax.random.split(key, 48)
    it = iter([ks[i] for i in range(48)])
    f32 = jnp.float32

    def nrm(shape, scale):
        return scale * jax.random.normal(next(it), shape, f32)

    n_pages = PAST_LEN // PAGE_SIZE
    n_used = DEC_BATCH * n_pages
    n_phys = n_used + n_used // 4
    x_prompt = nrm((BATCH, SEQ, D_MODEL), 1.0)
    x_sample = nrm((DEC_BATCH, DEC_SEQ, D_MODEL), 1.0)
    mem_prompt = nrm((BATCH, N_MEM, D_MODEL), 1.0)
    cache_diff_k = nrm((DEPTH, n_phys, PAGE_SIZE, DIFF_HEADS, 2, DIFF_DH), 1.0)
    cache_diff_v = nrm((DEPTH, n_phys, PAGE_SIZE, DIFF_HEADS, 2 * DIFF_DH), 1.0)
    page_table = jax.random.permutation(next(it), n_phys)[:n_used].reshape(DEC_BATCH, n_pages).astype(jnp.int32)
    cache_mem_k = nrm((DEPTH, DEC_BATCH, N_MEM, X_HEADS, X_DH), 1.0)
    cache_mem_v = nrm((DEPTH, DEC_BATCH, N_MEM, X_HEADS, X_DH), 1.0)
    state_conv = nrm((DEPTH, DEC_BATCH, CONV_W - 1, CONV_CH), 1.0)
    state_ssm_re = nrm((DEPTH, DEC_BATCH, SSM_GROUPS, SSM_P), 0.3)
    state_ssm_im = nrm((DEPTH, DEC_BATCH, SSM_GROUPS, SSM_P), 0.3)
    norm_mix = 1.0 + nrm((DEPTH, D_MODEL), 0.05)
    w_in = nrm((DEPTH, D_MODEL, IN_COLS), D_MODEL ** -0.5)
    conv_w = nrm((DEPTH, CONV_W, CONV_CH), 0.5)
    lambda_q1 = nrm((DEPTH, DIFF_DH), 0.1)
    lambda_k1 = nrm((DEPTH, DIFF_DH), 0.1)
    lambda_q2 = nrm((DEPTH, DIFF_DH), 0.1)
    lambda_k2 = nrm((DEPTH, DIFF_DH), 0.1)
    subln_w = 1.0 + nrm((DEPTH, 2 * DIFF_DH), 0.05)
    ssm_a_re = -0.5 + nrm((DEPTH, SSM_GROUPS, SSM_P), 0.01)
    ssm_a_im = math.pi * jnp.arange(SSM_P, dtype=f32)[None, None, :] + nrm((DEPTH, SSM_GROUPS, SSM_P), 0.01)
    ssm_log_dt = jax.random.uniform(next(it), (DEPTH, SSM_GROUPS), f32, math.log(1e-3), math.log(1e-1))
    ssm_b_re = nrm((DEPTH, SSM_GROUPS, SSM_P, SSM_GROUP), (2 * SSM_GROUP) ** -0.5)
    ssm_b_im = nrm((DEPTH, SSM_GROUPS, SSM_P, SSM_GROUP), (2 * SSM_GROUP) ** -0.5)
    ssm_c_re = nrm((DEPTH, SSM_GROUPS, SSM_GROUP, SSM_P), SSM_P ** -0.5)
    ssm_c_im = nrm((DEPTH, SSM_GROUPS, SSM_GROUP, SSM_P), SSM_P ** -0.5)
    ssm_d = nrm((DEPTH, SSM_CH), 0.5)
    ssm_w_glu = nrm((DEPTH, SSM_CH, SSM_CH), SSM_CH ** -0.5)
    w_o = nrm((DEPTH, MIX_WIDTH, D_MODEL), MIX_WIDTH ** -0.5)
    norm_x = 1.0 + nrm((DEPTH, D_MODEL), 0.05)
    norm_mem = 1.0 + nrm((DEPTH, D_MODEL), 0.05)
    w_xq = nrm((DEPTH, D_MODEL, D_MODEL), D_MODEL ** -0.5)
    w_xk = nrm((DEPTH, D_MODEL, D_MODEL), D_MODEL ** -0.5)
    w_xv = nrm((DEPTH, D_MODEL, D_MODEL), D_MODEL ** -0.5)
    w_xo = nrm((DEPTH, D_MODEL, D_MODEL), D_MODEL ** -0.5)
    norm_mlp = 1.0 + nrm((DEPTH, D_MODEL), 0.05)
    w_up = nrm((DEPTH, D_MODEL, D_FF), D_MODEL ** -0.5)
    w_down = nrm((DEPTH, D_FF, D_MODEL), D_FF ** -0.5)
    norm_final = 1.0 + nrm((D_MODEL,), 0.05)
    return {'x_prompt': x_prompt, 'x_sample': x_sample, 'mem_prompt': mem_prompt,
            'cache_diff_k': cache_diff_k, 'cache_diff_v': cache_diff_v, 'page_table': page_table,
            'cache_mem_k': cache_mem_k, 'cache_mem_v': cache_mem_v, 'state_conv': state_conv,
            'state_ssm_re': state_ssm_re, 'state_ssm_im': state_ssm_im,
            'norm_mix': norm_mix, 'w_in': w_in, 'conv_w': conv_w,
            'lambda_q1': lambda_q1, 'lambda_k1': lambda_k1, 'lambda_q2': lambda_q2, 'lambda_k2': lambda_k2,
            'subln_w': subln_w, 'ssm_a_re': ssm_a_re, 'ssm_a_im': ssm_a_im, 'ssm_log_dt': ssm_log_dt,
            'ssm_b_re': ssm_b_re, 'ssm_b_im': ssm_b_im, 'ssm_c_re': ssm_c_re, 'ssm_c_im': ssm_c_im,
            'ssm_d': ssm_d, 'ssm_w_glu': ssm_w_glu, 'w_o': w_o,
            'norm_x': norm_x, 'norm_mem': norm_mem, 'w_xq': w_xq, 'w_xk': w_xk, 'w_xv': w_xv, 'w_xo': w_xo,
            'norm_mlp': norm_mlp, 'w_up': w_up, 'w_down': w_down, 'norm_final': norm_final}


def reference(x_prompt, x_sample, mem_prompt, cache_diff_k, cache_diff_v, page_table,
              cache_mem_k, cache_mem_v, state_conv, state_ssm_re, state_ssm_im,
              norm_mix, w_in, conv_w, lambda_q1, lambda_k1, lambda_q2, lambda_k2, subln_w,
              ssm_a_re, ssm_a_im, ssm_log_dt, ssm_b_re, ssm_b_im, ssm_c_re, ssm_c_im, ssm_d, ssm_w_glu,
              w_o, norm_x, norm_mem, w_xq, w_xk, w_xv, w_xo, norm_mlp, w_up, w_down, norm_final):
    W = {'norm_mix': norm_mix, 'w_in': w_in, 'conv_w': conv_w,
         'lambda_q1': lambda_q1, 'lambda_k1': lambda_k1, 'lambda_q2': lambda_q2, 'lambda_k2': lambda_k2,
         'subln_w': subln_w, 'ssm_a_re': ssm_a_re, 'ssm_a_im': ssm_a_im, 'ssm_log_dt': ssm_log_dt,
         'ssm_b_re': ssm_b_re, 'ssm_b_im': ssm_b_im, 'ssm_c_re': ssm_c_re, 'ssm_c_im': ssm_c_im,
         'ssm_d': ssm_d, 'ssm_w_glu': ssm_w_glu, 'w_o': w_o,
         'norm_x': norm_x, 'w_xq': w_xq, 'w_xo': w_xo, 'norm_mlp': norm_mlp, 'w_up': w_up, 'w_down': w_down}

    Bp, Lp, _ = x_prompt.shape
    pos_p = jnp.arange(Lp, dtype=jnp.int32)
    conv0 = jnp.zeros((Bp, CONV_W - 1, CONV_CH), x_prompt.dtype)
    h0 = jnp.zeros((Bp, SSM_GROUPS, SSM_P), jnp.float32)
    xp = x_prompt
    kp, vp, mkp, mvp, cp, hrp, hip = [], [], [], [], [], [], []
    for l in range(DEPTH):
        mk, mv = mem_kv(mem_prompt, norm_mem[l], w_xk[l], w_xv[l])
        xp, k_l, v_l, c_l, hr_l, hi_l = hybrid_layer(l, W, xp, pos_p, conv0, h0, h0, mk, mv, None, None)
        kp.append(k_l); vp.append(v_l); mkp.append(mk); mvp.append(mv)
        cp.append(c_l); hrp.append(hr_l); hip.append(hi_l)
    y_prompt = rmsnorm(xp, norm_final)

    Bs, Ls, _ = x_sample.shape
    past_len = page_table.shape[1] * PAGE_SIZE
    pos_s = past_len + jnp.arange(Ls, dtype=jnp.int32)
    xs = x_sample
    ks_, vs_, cs_, hrs, his = [], [], [], [], []
    for l in range(DEPTH):
        k_past = cache_diff_k[l][page_table].reshape(Bs, past_len, DIFF_HEADS, 2, DIFF_DH)
        v_past = cache_diff_v[l][page_table].reshape(Bs, past_len, DIFF_HEADS, 2 * DIFF_DH)
        xs, k_l, v_l, c_l, hr_l, hi_l = hybrid_layer(l, W, xs, pos_s, state_conv[l], state_ssm_re[l], state_ssm_im[l],
                                                     cache_mem_k[l], cache_mem_v[l], k_past, v_past)
        ks_.append(k_l); vs_.append(v_l); cs_.append(c_l); hrs.append(hr_l); his.append(hi_l)
    y_sample = rmsnorm(xs, norm_final)

    new_diff_k_prompt = jnp.stack(kp)
    new_diff_v_prompt = jnp.stack(vp)
    new_mem_k_prompt = jnp.stack(mkp)
    new_mem_v_prompt = jnp.stack(mvp)
    new_conv_prompt = jnp.stack(cp)
    new_ssm_re_prompt = jnp.stack(hrp)
    new_ssm_im_prompt = jnp.stack(hip)
    new_diff_k_sample = jnp.stack(ks_)
    new_diff_v_sample = jnp.stack(vs_)
    new_conv_sample = jnp.stack(cs_)
    new_ssm_re_sample = jnp.stack(hrs)
    new_ssm_im_sample = jnp.stack(his)
    return (y_prompt, y_sample, new_diff_k_prompt, new_diff_v_prompt, new_mem_k_prompt, new_mem_v_prompt,
            new_conv_prompt, new_ssm_re_prompt, new_ssm_im_prompt,
            new_diff_k_sample, new_diff_v_sample, new_conv_sample, new_ssm_re_sample, new_ssm_im_sample)
```

```python
import functools
import math

import jax
import jax.numpy as jnp
from jax import lax
from jax.experimental import pallas as pl
from jax.experimental.pallas import tpu as pltpu

F32 = jnp.float32
BF16 = jnp.bfloat16

D_MODEL = 1024
DEPTH = 4
PAGE_SIZE = 128
N_MEM = 256
CONV_CH = 256
CONV_W = 3
DIFF_HEADS = 4
DIFF_DH = 64
HEAD_W = 2 * DIFF_DH
DIFF_WIDTH = DIFF_HEADS * HEAD_W
SSM_CH = 256
SSM_GROUP = 16
SSM_GROUPS = 16
SSM_P = 64
SSM_STATE = SSM_GROUPS * SSM_P
X_HEADS = 4
X_DH = 256
D_FF = 4096
ROPE_THETA = 500000.0
ROPE_DIM = 16
ROPE_HALF = 8
EPS = 1e-6
COL_Q = 3 * CONV_CH
COL_K = COL_Q + DIFF_WIDTH
COL_V = COL_K + DIFF_WIDTH
COL_U = COL_V + DIFF_WIDTH
IN_COLS = COL_U + SSM_CH
S5_BATCH = 8
MIB = 1024 * 1024


def _cparams(semantics, vmem_mib):
    return pltpu.CompilerParams(dimension_semantics=semantics,
                                vmem_limit_bytes=vmem_mib * MIB)


def _rms(x, g):
    return x * lax.rsqrt(jnp.mean(x * x, axis=-1, keepdims=True) + EPS) * g


def _dot(a, b):
    return jnp.dot(a, b, preferred_element_type=F32)


def _dot_nt(a, b):
    return lax.dot_general(a, b, (((1,), (1,)), ((), ())), preferred_element_type=F32)


def _rope_lanes(z, c, s1, s2):
    return z * c + pltpu.roll(z, HEAD_W - ROPE_HALF, 1) * s1 + pltpu.roll(z, ROPE_HALF, 1) * s2


def _in_proj_kernel(x_ref, g_ref, w_ref, c_ref, s1_ref, s2_ref,
                    bcx_ref, q_ref, k_ref, kb_ref, v_ref, vb_ref, u_ref):
    h = _rms(x_ref[...], g_ref[...]).astype(BF16)
    bcx_ref[...] = _dot(h, w_ref[:, 0:COL_Q])
    c = c_ref[...]
    s1 = s1_ref[...]
    s2 = s2_ref[...]
    zq = _dot(h, w_ref[:, COL_Q:COL_K])
    zk = _dot(h, w_ref[:, COL_K:COL_V])
    for hd in range(DIFF_HEADS):
        sl = slice(hd * HEAD_W, (hd + 1) * HEAD_W)
        q_ref[:, sl] = (_rope_lanes(zq[:, sl], c, s1, s2) * (1.0 / math.sqrt(DIFF_DH))).astype(BF16)
        kr = _rope_lanes(zk[:, sl], c, s1, s2)
        k_ref[:, sl] = kr
        kb_ref[:, sl] = kr.astype(BF16)
    v = _dot(h, w_ref[:, COL_V:COL_U])
    v_ref[...] = v
    vb_ref[...] = v.astype(BF16)
    u_ref[...] = _dot(h, w_ref[:, COL_U:IN_COLS])


def _in_proj(x, g, w_bf, tabs, tm, pos_blocks):
    m = x.shape[0]
    c, s1, s2 = tabs
    row = lambda i: (i, 0)
    const = lambda i: (0, 0)
    tab = lambda i: (i % pos_blocks, 0)
    out_shapes = (
        jax.ShapeDtypeStruct((m, COL_Q), F32),
        jax.ShapeDtypeStruct((m, DIFF_WIDTH), BF16),
        jax.ShapeDtypeStruct((m, DIFF_WIDTH), F32),
        jax.ShapeDtypeStruct((m, DIFF_WIDTH), BF16),
        jax.ShapeDtypeStruct((m, DIFF_WIDTH), F32),
        jax.ShapeDtypeStruct((m, DIFF_WIDTH), BF16),
        jax.ShapeDtypeStruct((m, SSM_CH), F32),
    )
    return pl.pallas_call(
        _in_proj_kernel,
        grid=(m // tm,),
        in_specs=[pl.BlockSpec((tm, D_MODEL), row),
                  pl.BlockSpec((1, D_MODEL), const),
                  pl.BlockSpec((D_MODEL, IN_COLS), const),
                  pl.BlockSpec((tm, HEAD_W), tab),
                  pl.BlockSpec((tm, HEAD_W), tab),
                  pl.BlockSpec((tm, HEAD_W), tab)],
        out_specs=[pl.BlockSpec((tm, COL_Q), row),
                   pl.BlockSpec((tm, DIFF_WIDTH), row),
                   pl.BlockSpec((tm, DIFF_WIDTH), row),
                   pl.BlockSpec((tm, DIFF_WIDTH), row),
                   pl.BlockSpec((tm, DIFF_WIDTH), row),
                   pl.BlockSpec((tm, DIFF_WIDTH), row),
                   pl.BlockSpec((tm, SSM_CH), row)],
        out_shape=out_shapes,
        compiler_params=_cparams(("parallel",), 56),
        name="in_proj",
    )(x, g, w_bf, c, s1, s2)


def _rope_tables(pos):
    inv_freq = ROPE_THETA ** (-jnp.arange(ROPE_HALF, dtype=F32) / ROPE_HALF)
    ang = pos.astype(F32)[:, None] * inv_freq[None, :]
    cos = jnp.cos(ang)
    sin = jnp.sin(ang)
    n = pos.shape[0]
    ones = jnp.ones((n, DIFF_DH - ROPE_DIM), F32)
    zeros = jnp.zeros((n, DIFF_DH - ROPE_DIM), F32)
    z8 = jnp.zeros((n, ROPE_HALF), F32)
    c = jnp.concatenate([cos, cos, ones], axis=1)
    s1 = jnp.concatenate([-sin, z8, zeros], axis=1)
    s2 = jnp.concatenate([z8, sin, zeros], axis=1)
    two = lambda t: jnp.concatenate([t, t], axis=1)
    return two(c), two(s1), two(s2)


def _conv_prompt_kernel(bcx_ref, w_ref, ya_ref, st_ref):
    b = bcx_ref[:, 0:CONV_CH]
    u = bcx_ref[:, CONV_CH:2 * CONV_CH] * bcx_ref[:, 2 * CONV_CH:3 * CONV_CH]
    n = u.shape[0]
    row = lax.broadcasted_iota(jnp.int32, u.shape, 0)
    u1 = jnp.where(row >= 1, pltpu.roll(u, 1, 0), 0.0)
    u2 = jnp.where(row >= 2, pltpu.roll(u, 2, 0), 0.0)
    w = w_ref[...]
    ya_ref[...] = b * (w[0:1] * u2 + w[1:2] * u1 + w[2:3] * u)
    st_ref[...] = u[n - (CONV_W - 1):n, :]


def _conv_prompt(bcx, w, nb, seq):
    return pl.pallas_call(
        _conv_prompt_kernel,
        grid=(nb,),
        in_specs=[pl.BlockSpec((seq, COL_Q), lambda b: (b, 0)),
                  pl.BlockSpec((CONV_W, CONV_CH), lambda b: (0, 0))],
        out_specs=[pl.BlockSpec((seq, CONV_CH), lambda b: (b, 0)),
                   pl.BlockSpec((None, CONV_W - 1, CONV_CH), lambda b: (b, 0, 0))],
        out_shape=(jax.ShapeDtypeStruct((nb * seq, CONV_CH), F32),
                   jax.ShapeDtypeStruct((nb, CONV_W - 1, CONV_CH), F32)),
        compiler_params=_cparams(("parallel",), 48),
        name="conv_prompt",
    )(bcx, w)


def _conv_sample_kernel(bcx_ref, st_ref, w_ref, ya_ref, nst_ref):
    steps = bcx_ref.shape[0]
    ext = [st_ref[0], st_ref[1]]
    for t in range(steps):
        ext.append(bcx_ref[t, :, CONV_CH:2 * CONV_CH] * bcx_ref[t, :, 2 * CONV_CH:3 * CONV_CH])
    w = w_ref[...]
    for t in range(steps):
        y = w[0:1] * ext[t] + w[1:2] * ext[t + 1] + w[2:3] * ext[t + 2]
        ya_ref[t] = bcx_ref[t, :, 0:CONV_CH] * y
    nst_ref[0] = ext[steps]
    nst_ref[1] = ext[steps + 1]


def _conv_sample(bcx_t, st_t, w):
    steps, nb, _ = bcx_t.shape
    return pl.pallas_call(
        _conv_sample_kernel,
        out_shape=(jax.ShapeDtypeStruct((steps, nb, CONV_CH), F32),
                   jax.ShapeDtypeStruct((CONV_W - 1, nb, CONV_CH), F32)),
        name="conv_sample",
    )(bcx_t, st_t, w)


def _lambda(lq1_ref, lk1_ref, lq2_ref, lk2_ref, lam_init):
    a = jnp.sum(lq1_ref[...] * lk1_ref[...], axis=-1, keepdims=True)
    b = jnp.sum(lq2_ref[...] * lk2_ref[...], axis=-1, keepdims=True)
    return jnp.exp(a) - jnp.exp(b) + lam_init


def _attn_prompt_kernel(lq1_ref, lk1_ref, lq2_ref, lk2_ref, sub_ref, q_ref, k_ref, v_ref,
                        o_ref, m_sc, l_sc, acc_sc, *, lam_init, tq):
    i = pl.program_id(2)
    q = q_ref[...]
    lane = lax.broadcasted_iota(jnp.int32, q.shape, 1)
    zero = jnp.zeros_like(q)
    qc = (jnp.where(lane < DIFF_DH, q, zero), jnp.where(lane >= DIFF_DH, q, zero))
    m_sc[...] = jnp.full(m_sc.shape, -jnp.inf, F32)
    l_sc[...] = jnp.zeros(l_sc.shape, F32)
    acc_sc[...] = jnp.zeros(acc_sc.shape, F32)

    def block(j, masked):
        off = pl.multiple_of(j * tq, tq)
        ks = k_ref[pl.ds(off, tq), :]
        vs = v_ref[pl.ds(off, tq), :]
        for c in range(2):
            s = _dot_nt(qc[c], ks)
            if masked:
                r = lax.broadcasted_iota(jnp.int32, s.shape, 0)
                col = lax.broadcasted_iota(jnp.int32, s.shape, 1)
                s = jnp.where(col <= r, s, -jnp.inf)
            m_prev = m_sc[c]
            m_new = jnp.maximum(m_prev, jnp.max(s, axis=-1, keepdims=True))
            p = jnp.exp(s - m_new)
            alpha = jnp.exp(m_prev - m_new)
            l_sc[c] = alpha * l_sc[c] + jnp.sum(p, axis=-1, keepdims=True)
            acc_sc[c] = alpha * acc_sc[c] + _dot(p.astype(BF16), vs)
            m_sc[c] = m_new

    def body(j, carry):
        block(j, False)
        return carry

    lax.fori_loop(0, i, body, 0)
    block(i, True)

    lam = _lambda(lq1_ref, lk1_ref, lq2_ref, lk2_ref, lam_init)
    o = acc_sc[0] / l_sc[0] - lam * (acc_sc[1] / l_sc[1])
    o_ref[...] = _rms(o, sub_ref[...]) * (1.0 - lam_init)


def _attn_prompt(lams, sub, q_bf, k_bf, v_bf, nb, seq, lam_init, tq):
    nq = seq // tq
    small = lambda b, h, i: (0, 0)
    kernel = functools.partial(_attn_prompt_kernel, lam_init=lam_init, tq=tq)
    return pl.pallas_call(
        kernel,
        grid=(nb, DIFF_HEADS, nq),
        in_specs=[pl.BlockSpec((1, DIFF_DH), small)] * 4 + [
            pl.BlockSpec((1, HEAD_W), small),
            pl.BlockSpec((tq, HEAD_W), lambda b, h, i: (b * nq + i, h)),
            pl.BlockSpec((seq, HEAD_W), lambda b, h, i: (b, h)),
            pl.BlockSpec((seq, HEAD_W), lambda b, h, i: (b, h))],
        out_specs=pl.BlockSpec((tq, HEAD_W), lambda b, h, i: (b * nq + i, h)),
        out_shape=jax.ShapeDtypeStruct((nb * seq, DIFF_WIDTH), F32),
        scratch_shapes=[pltpu.VMEM((2, tq, 1), F32),
                        pltpu.VMEM((2, tq, 1), F32),
                        pltpu.VMEM((2, tq, HEAD_W), F32)],
        compiler_params=_cparams(("parallel", "parallel", "arbitrary"), 48),
        name="attn_prompt",
    )(*lams, sub, q_bf, k_bf, v_bf)


def _attn_sample_kernel(pt_ref, lq1_ref, lk1_ref, lq2_ref, lk2_ref, sub_ref,
                        qbd_ref, kn_ref, vn_ref, *rest, lam_init, n_pages):
    k_refs = rest[:n_pages]
    v_refs = rest[n_pages:2 * n_pages]
    o_ref = rest[2 * n_pages]
    s_sc = rest[2 * n_pages + 1]
    del pt_ref
    qbd = qbd_ref[...]
    rows = qbd.shape[0]
    half = rows // 2
    steps = half // DIFF_HEADS
    for j in range(n_pages):
        kt = k_refs[j][...].reshape(DIFF_WIDTH, PAGE_SIZE).astype(BF16)
        s_sc[:, j * PAGE_SIZE:(j + 1) * PAGE_SIZE] = _dot(qbd, kt)
    s = s_sc[...]
    s_new = _dot_nt(qbd, kn_ref[...])
    r = lax.broadcasted_iota(jnp.int32, s_new.shape, 0)
    col = lax.broadcasted_iota(jnp.int32, s_new.shape, 1)
    s_new = jnp.where(col <= r % steps, s_new, -jnp.inf)
    m = jnp.maximum(jnp.max(s, axis=-1, keepdims=True), jnp.max(s_new, axis=-1, keepdims=True))
    p = jnp.exp(s - m)
    pn = jnp.exp(s_new - m)
    l = jnp.sum(p, axis=-1, keepdims=True) + jnp.sum(pn, axis=-1, keepdims=True)
    lam = _lambda(lq1_ref, lk1_ref, lq2_ref, lk2_ref, lam_init)
    inv = 1.0 / l
    a = (p[0:half] * inv[0:half] - lam * (p[half:rows] * inv[half:rows])).astype(BF16)
    an = (pn[0:half] * inv[0:half] - lam * (pn[half:rows] * inv[half:rows])).astype(BF16)
    rg = lax.broadcasted_iota(jnp.int32, (half, HEAD_W), 0) // steps
    out = jnp.zeros((half, HEAD_W), F32)
    vn = vn_ref[...]
    for hd in range(DIFF_HEADS):
        acc = _dot(an, vn[:, hd * HEAD_W:(hd + 1) * HEAD_W])
        for j in range(n_pages):
            vh = v_refs[j][pl.ds(hd, PAGE_SIZE, stride=DIFF_HEADS), :].astype(BF16)
            acc = acc + _dot(a[:, j * PAGE_SIZE:(j + 1) * PAGE_SIZE], vh)
        out = jnp.where(rg == hd, acc, out)
    o_ref[...] = _rms(out, sub_ref[...]) * (1.0 - lam_init)


def _attn_sample(page_table, lams, sub, qbd, kn8, vn8, ck_t, cv_r, layer, lam_init):
    nb, n_pages = page_table.shape
    rows = qbd.shape[1]
    small = lambda b, pt: (0, 0)
    per_b = lambda b, pt: (b, 0, 0)

    def k_map(j):
        return lambda b, pt: (layer, pt[b, j], 0, 0, 0, 0)

    def v_map(j):
        return lambda b, pt: (layer, pt[b, j], 0, 0)

    in_specs = [pl.BlockSpec((1, DIFF_DH), small)] * 4 + [
        pl.BlockSpec((1, HEAD_W), small),
        pl.BlockSpec((None, rows, DIFF_WIDTH), per_b),
        pl.BlockSpec((None, 8, DIFF_WIDTH), per_b),
        pl.BlockSpec((None, 8, DIFF_WIDTH), per_b)]
    in_specs += [pl.BlockSpec((None, None, DIFF_HEADS, 2, DIFF_DH, PAGE_SIZE), k_map(j)) for j in range(n_pages)]
    in_specs += [pl.BlockSpec((None, None, PAGE_SIZE * DIFF_HEADS, HEAD_W), v_map(j)) for j in range(n_pages)]
    kernel = functools.partial(_attn_sample_kernel, lam_init=lam_init, n_pages=n_pages)
    grid_spec = pltpu.PrefetchScalarGridSpec(
        num_scalar_prefetch=1,
        grid=(nb,),
        in_specs=in_specs,
        out_specs=pl.BlockSpec((None, rows // 2, HEAD_W), per_b),
        scratch_shapes=[pltpu.VMEM((rows, n_pages * PAGE_SIZE), F32)],
    )
    return pl.pallas_call(
        kernel,
        grid_spec=grid_spec,
        out_shape=jax.ShapeDtypeStruct((nb, rows // 2, HEAD_W), F32),
        compiler_params=_cparams(("arbitrary",), 48),
        name="attn_sample",
    )(page_table, *lams, sub, qbd, kn8, vn8, *([ck_t] * n_pages), *([cv_r] * n_pages))


def _s5_kernel(u_ref, h0r_ref, h0i_ref, are_ref, aim_ref, ldt_ref, bre_ref, bim_ref,
               cre_ref, cim_ref, d_ref, wg_ref, y_ref, hr_ref, hi_ref,
               xr_sc, xi_sc, hr_sc, hi_sc, *, tl):
    j = pl.program_id(1)
    nb = S5_BATCH

    @pl.when(j == 0)
    def _():
        hr_sc[...] = h0r_ref[...]
        hi_sc[...] = h0i_ref[...]

    a_re = are_ref[...]
    a_im = aim_ref[...]
    dt = jnp.exp(ldt_ref[...])
    mag = jnp.exp(a_re * dt)
    ab_re = mag * jnp.cos(a_im * dt)
    ab_im = mag * jnp.sin(a_im * dt)
    den = a_re * a_re + a_im * a_im
    nr = ab_re - 1.0
    g_re = (nr * a_re + ab_im * a_im) / den
    g_im = (ab_im * a_re - nr * a_im) / den

    u = u_ref[...].reshape(tl * nb, SSM_CH)
    ub = u.astype(BF16)
    bu_re = _dot(ub, bre_ref[...])
    bu_im = _dot(ub, bim_ref[...])
    xr_sc[...] = g_re * bu_re - g_im * bu_im
    xi_sc[...] = g_re * bu_im + g_im * bu_re

    abr = jnp.broadcast_to(ab_re, (nb, SSM_STATE))
    abi = jnp.broadcast_to(ab_im, (nb, SSM_STATE))

    def step(t, carry):
        hr, hi = carry
        off = pl.multiple_of(t * nb, nb)
        nhr = abr * hr - abi * hi + xr_sc[pl.ds(off, nb), :]
        nhi = abr * hi + abi * hr + xi_sc[pl.ds(off, nb), :]
        xr_sc[pl.ds(off, nb), :] = nhr
        xi_sc[pl.ds(off, nb), :] = nhi
        return nhr, nhi

    hr, hi = lax.fori_loop(0, tl, step, (hr_sc[...], hi_sc[...]))
    hr_sc[...] = hr
    hi_sc[...] = hi

    @pl.when(j == pl.num_programs(1) - 1)
    def _():
        hr_ref[...] = hr
        hi_ref[...] = hi

    y = _dot(xr_sc[...].astype(BF16), cre_ref[...]) - _dot(xi_sc[...].astype(BF16), cim_ref[...])
    y = y + d_ref[...] * u
    y = jax.nn.gelu(y)
    y = y * jax.nn.sigmoid(_dot(y.astype(BF16), wg_ref[...]))
    y_ref[...] = y.reshape(tl, nb, SSM_CH)


def _s5(u_t, h0r, h0i, prm, tl):
    steps, nbt, _ = u_t.shape
    nb = S5_BATCH
    const = lambda b, j: (0, 0)
    kernel = functools.partial(_s5_kernel, tl=tl)
    return pl.pallas_call(
        kernel,
        grid=(nbt // nb, steps // tl),
        in_specs=[pl.BlockSpec((tl, nb, SSM_CH), lambda b, j: (j, b, 0)),
                  pl.BlockSpec((nb, SSM_STATE), lambda b, j: (b, 0)),
                  pl.BlockSpec((nb, SSM_STATE), lambda b, j: (b, 0)),
                  pl.BlockSpec((1, SSM_STATE), const),
                  pl.BlockSpec((1, SSM_STATE), const),
                  pl.BlockSpec((1, SSM_STATE), const),
                  pl.BlockSpec((SSM_CH, SSM_STATE), const),
                  pl.BlockSpec((SSM_CH, SSM_STATE), const),
                  pl.BlockSpec((SSM_STATE, SSM_CH), const),
                  pl.BlockSpec((SSM_STATE, SSM_CH), const),
                  pl.BlockSpec((1, SSM_CH), const),
                  pl.BlockSpec((SSM_CH, SSM_CH), const)],
        out_specs=[pl.BlockSpec((tl, nb, SSM_CH), lambda b, j: (j, b, 0)),
                   pl.BlockSpec((nb, SSM_STATE), lambda b, j: (b, 0)),
                   pl.BlockSpec((nb, SSM_STATE), lambda b, j: (b, 0))],
        out_shape=(jax.ShapeDtypeStruct((steps, nbt, SSM_CH), F32),
                   jax.ShapeDtypeStruct((nbt, SSM_STATE), F32),
                   jax.ShapeDtypeStruct((nbt, SSM_STATE), F32)),
        scratch_shapes=[pltpu.VMEM((tl * nb, SSM_STATE), F32),
                        pltpu.VMEM((tl * nb, SSM_STATE), F32),
                        pltpu.VMEM((nb, SSM_STATE), F32),
                        pltpu.VMEM((nb, SSM_STATE), F32)],
        compiler_params=_cparams(("parallel", "arbitrary"), 48),
        name="s5",
    )(u_t, h0r, h0i, *prm)


def _s5_params(l, a_re, a_im, log_dt, b_re, b_im, c_re, c_im, d, w_glu):
    eye = jnp.eye(SSM_GROUPS, dtype=F32)
    flat = lambda a: a[l].reshape(1, SSM_STATE)
    ldt = jnp.repeat(log_dt[l], SSM_P).reshape(1, SSM_STATE)

    def b_dense(b):
        return jnp.einsum('gph,gk->ghkp', b[l], eye).reshape(SSM_CH, SSM_STATE).astype(BF16)

    def c_dense(c):
        return jnp.einsum('ghp,gk->gpkh', c[l], eye).reshape(SSM_STATE, SSM_CH).astype(BF16)

    return (flat(a_re), flat(a_im), ldt, b_dense(b_re), b_dense(b_im), c_dense(c_re), c_dense(c_im),
            d[l].reshape(1, SSM_CH), w_glu[l].astype(BF16))


def _residual_proj_kernel(x_ref, *rest):
    ys = rest[:-2]
    w_ref, o_ref = rest[-2:]
    y = jnp.concatenate([r[...] for r in ys], axis=-1) if len(ys) > 1 else ys[0][...]
    o_ref[...] = x_ref[...] + _dot(y.astype(BF16), w_ref[...])


def _residual_proj(x, ys, w_bf, tm):
    m = x.shape[0]
    row = lambda i: (i, 0)
    return pl.pallas_call(
        _residual_proj_kernel,
        grid=(m // tm,),
        in_specs=[pl.BlockSpec((tm, D_MODEL), row)]
        + [pl.BlockSpec((tm, y.shape[1]), row) for y in ys]
        + [pl.BlockSpec(w_bf.shape, lambda i: (0, 0))],
        out_specs=pl.BlockSpec((tm, D_MODEL), row),
        out_shape=jax.ShapeDtypeStruct((m, D_MODEL), F32),
        compiler_params=_cparams(("parallel",), 48),
        name="residual_proj",
    )(x, *ys, w_bf)


def _norm_proj_kernel(x_ref, g_ref, *rest, scale):
    n = len(rest) // 3
    h = _rms(x_ref[...], g_ref[...]).astype(BF16)
    for w_ref, o_ref, ob_ref in zip(rest[:n], rest[n:2 * n], rest[2 * n:]):
        y = _dot(h, w_ref[...])
        o_ref[...] = y
        ob_ref[...] = (y * scale).astype(BF16)


def _norm_proj(x, g, ws, tm, scale=1.0):
    m = x.shape[0]
    row = lambda i: (i, 0)
    const = lambda i: (0, 0)
    outs = ([jax.ShapeDtypeStruct((m, w.shape[1]), F32) for w in ws]
            + [jax.ShapeDtypeStruct((m, w.shape[1]), BF16) for w in ws])
    return pl.pallas_call(
        functools.partial(_norm_proj_kernel, scale=scale),
        grid=(m // tm,),
        in_specs=[pl.BlockSpec((tm, D_MODEL), row), pl.BlockSpec((1, D_MODEL), const)]
        + [pl.BlockSpec(w.shape, const) for w in ws],
        out_specs=[pl.BlockSpec((tm, w.shape[1]), row) for w in ws] * 2,
        out_shape=outs,
        compiler_params=_cparams(("parallel",), 48),
        name="norm_proj",
    )(x, g, *ws)


def _xattn_heads(qs, kb, vb, mask=None):
    outs = []
    for hd in range(X_HEADS):
        sl = slice(hd * X_DH, (hd + 1) * X_DH)
        s = _dot_nt(qs[:, sl], kb[:, sl])
        if mask is not None:
            s = jnp.where(mask, s, -jnp.inf)
        p = jnp.exp(s - jnp.max(s, axis=-1, keepdims=True))
        l = jnp.sum(p, axis=-1, keepdims=True)
        outs.append(_dot(p.astype(BF16), vb[:, sl]) / l)
    return jnp.concatenate(outs, axis=-1)


def _xattn_prompt_kernel(x_ref, g_ref, wq_ref, mk_ref, mv_ref, wo_ref, o_ref):
    x = x_ref[...]
    h = _rms(x, g_ref[...]).astype(BF16)
    qs = (_dot(h, wq_ref[...]) * (1.0 / math.sqrt(X_DH))).astype(BF16)
    o = _xattn_heads(qs, mk_ref[...], mv_ref[...])
    o_ref[...] = x + _dot(o.astype(BF16), wo_ref[...])


def _xattn_prompt(x, g, wq, mk_bf, mv_bf, wo, nb, seq, tq):
    nq = seq // tq
    row = lambda b, i: (b * nq + i, 0)
    const = lambda b, i: (0, 0)
    mem = lambda b, i: (b, 0)
    return pl.pallas_call(
        _xattn_prompt_kernel,
        grid=(nb, nq),
        in_specs=[pl.BlockSpec((tq, D_MODEL), row),
                  pl.BlockSpec((1, D_MODEL), const),
                  pl.BlockSpec((D_MODEL, D_MODEL), const),
                  pl.BlockSpec((N_MEM, D_MODEL), mem),
                  pl.BlockSpec((N_MEM, D_MODEL), mem),
                  pl.BlockSpec((D_MODEL, D_MODEL), const)],
        out_specs=pl.BlockSpec((tq, D_MODEL), row),
        out_shape=jax.ShapeDtypeStruct(x.shape, F32),
        compiler_params=_cparams(("parallel", "parallel"), 48),
        name="xattn_prompt",
    )(x, g, wq, mk_bf, mv_bf, wo)


def _xattn_sample_kernel(q_ref, k_ref, v_ref, o_ref, *, steps):
    nseq = k_ref.shape[0]
    kb = k_ref[...].reshape(nseq * N_MEM, D_MODEL).astype(BF16)
    vb = v_ref[...].reshape(nseq * N_MEM, D_MODEL).astype(BF16)
    r = lax.broadcasted_iota(jnp.int32, (nseq * steps, nseq * N_MEM), 0)
    col = lax.broadcasted_iota(jnp.int32, (nseq * steps, nseq * N_MEM), 1)
    mask = (r // steps) == (col // N_MEM)
    o_ref[...] = _xattn_heads(q_ref[...], kb, vb, mask)


def _xattn_sample(q_bf, ck, cv, steps, nseq):
    m = q_bf.shape[0]
    nb = ck.shape[0]
    return pl.pallas_call(
        functools.partial(_xattn_sample_kernel, steps=steps),
        grid=(nb // nseq,),
        in_specs=[pl.BlockSpec((nseq * steps, D_MODEL), lambda i: (i, 0)),
                  pl.BlockSpec((nseq, N_MEM, D_MODEL), lambda i: (i, 0, 0)),
                  pl.BlockSpec((nseq, N_MEM, D_MODEL), lambda i: (i, 0, 0))],
        out_specs=pl.BlockSpec((nseq * steps, D_MODEL), lambda i: (i, 0)),
        out_shape=jax.ShapeDtypeStruct((m, D_MODEL), F32),
        compiler_params=_cparams(("parallel",), 48),
        name="xattn_sample",
    )(q_bf, ck, cv)


def _mlp_kernel(x_ref, g_ref, wu_ref, wd_ref, o_ref, h_sc):
    f = pl.program_id(1)

    @pl.when(f == 0)
    def _():
        h_sc[...] = _rms(x_ref[...], g_ref[...]).astype(BF16)
        o_ref[...] = jnp.zeros(o_ref.shape, F32)

    a = jnp.square(jnp.maximum(_dot(h_sc[...], wu_ref[...]), 0.0)).astype(BF16)
    o_ref[...] += _dot(a, wd_ref[...])

    @pl.when(f == pl.num_programs(1) - 1)
    def _():
        o_ref[...] = x_ref[...] + o_ref[...]


def _mlp(x, g, wu, wd, tm, tf):
    m = x.shape[0]
    return pl.pallas_call(
        _mlp_kernel,
        grid=(m // tm, D_FF // tf),
        in_specs=[pl.BlockSpec((tm, D_MODEL), lambda i, f: (i, 0)),
                  pl.BlockSpec((1, D_MODEL), lambda i, f: (0, 0)),
                  pl.BlockSpec((D_MODEL, tf), lambda i, f: (0, f)),
                  pl.BlockSpec((tf, D_MODEL), lambda i, f: (f, 0))],
        out_specs=pl.BlockSpec((tm, D_MODEL), lambda i, f: (i, 0)),
        out_shape=jax.ShapeDtypeStruct(x.shape, F32),
        scratch_shapes=[pltpu.VMEM((tm, D_MODEL), BF16)],
        compiler_params=_cparams(("parallel", "arbitrary"), 48),
        name="mlp",
    )(x, g, wu, wd)


def _final_norm_kernel(x_ref, g_ref, o_ref):
    o_ref[...] = _rms(x_ref[...], g_ref[...])


def _final_norm(x, g, tm):
    m = x.shape[0]
    return pl.pallas_call(
        _final_norm_kernel,
        grid=(m // tm,),
        in_specs=[pl.BlockSpec((tm, D_MODEL), lambda i: (i, 0)),
                  pl.BlockSpec((1, D_MODEL), lambda i: (0, 0))],
        out_specs=pl.BlockSpec((tm, D_MODEL), lambda i: (i, 0)),
        out_shape=jax.ShapeDtypeStruct(x.shape, F32),
        compiler_params=_cparams(("parallel",), 32),
        name="final_norm",
    )(x, g)


def kernel(x_prompt, x_sample, mem_prompt, cache_diff_k, cache_diff_v, page_table, cache_mem_k, cache_mem_v, state_conv, state_ssm_re, state_ssm_im, norm_mix, w_in, conv_w, lambda_q1, lambda_k1, lambda_q2, lambda_k2, subln_w, ssm_a_re, ssm_a_im, ssm_log_dt, ssm_b_re, ssm_b_im, ssm_c_re, ssm_c_im, ssm_d, ssm_w_glu, w_o, norm_x, norm_mem, w_xq, w_xk, w_xv, w_xo, norm_mlp, w_up, w_down, norm_final):
    nbp, seq, _ = x_prompt.shape
    nbs, steps, _ = x_sample.shape
    n_pages = page_table.shape[1]
    past = n_pages * PAGE_SIZE
    mp = nbp * seq
    ms = nbs * steps

    tabs_p = _rope_tables(jnp.arange(seq, dtype=jnp.int32))
    pos_s = past + jnp.arange(steps, dtype=jnp.int32)
    tabs_s = tuple(jnp.tile(t, (nbs, 1)) for t in _rope_tables(pos_s))

    ck_t = jnp.transpose(cache_diff_k, (0, 1, 3, 4, 5, 2))
    cv_r = cache_diff_v.reshape(DEPTH, cache_diff_v.shape[1], PAGE_SIZE * DIFF_HEADS, HEAD_W)

    xp = x_prompt.reshape(mp, D_MODEL)
    xs = x_sample.reshape(ms, D_MODEL)
    mem = mem_prompt.reshape(nbp * N_MEM, D_MODEL)
    zeros_state = jnp.zeros((nbp, SSM_STATE), F32)
    row1 = lambda a: a.reshape(1, -1)

    outs = {k: [] for k in ('kp', 'vp', 'mkp', 'mvp', 'cp', 'hrp', 'hip', 'ks', 'vs', 'cs', 'hrs', 'his')}
    for l in range(DEPTH):
        lam_init = 0.8 - 0.6 * math.exp(-0.3 * l)
        w_in_bf = w_in[l].astype(BF16)
        w_o_bf = w_o[l].astype(BF16)
        w_xq_bf = w_xq[l].astype(BF16)
        w_xo_bf = w_xo[l].astype(BF16)
        w_up_bf = w_up[l].astype(BF16)
        w_down_bf = w_down[l].astype(BF16)
        lams = (row1(lambda_q1[l]), row1(lambda_k1[l]), row1(lambda_q2[l]), row1(lambda_k2[l]))
        sub = row1(subln_w[l])
        s5p = _s5_params(l, ssm_a_re, ssm_a_im, ssm_log_dt, ssm_b_re, ssm_b_im, ssm_c_re, ssm_c_im,
                         ssm_d, ssm_w_glu)
        g_mix = row1(norm_mix[l])

        mk, mv, mk_bf, mv_bf = _norm_proj(mem, row1(norm_mem[l]),
                                          [w_xk[l].astype(BF16), w_xv[l].astype(BF16)], 512)
        bcx, q_bf, k, k_bf, v, v_bf, u = _in_proj(xp, g_mix, w_in_bf, tabs_p, 512, seq // 512)
        ya, cst = _conv_prompt(bcx, conv_w[l], nbp, seq)
        yb = _attn_prompt(lams, sub, q_bf, k_bf, v_bf, nbp, seq, lam_init, 512)
        u_t = u.reshape(nbp, seq, SSM_CH).transpose(1, 0, 2)
        yc_t, hr, hi = _s5(u_t, zeros_state, zeros_state, s5p, 128)
        yc = yc_t.transpose(1, 0, 2).reshape(mp, SSM_CH)
        xp = _residual_proj(xp, [ya, yb, yc], w_o_bf, 512)
        xp = _xattn_prompt(xp, row1(norm_x[l]), w_xq_bf, mk_bf, mv_bf, w_xo_bf, nbp, seq, 512)
        xp = _mlp(xp, row1(norm_mlp[l]), w_up_bf, w_down_bf, 1024, 512)
        outs['kp'].append(k.reshape(nbp, seq, DIFF_HEADS, 2, DIFF_DH))
        outs['vp'].append(v.reshape(nbp, seq, DIFF_HEADS, HEAD_W))
        outs['mkp'].append(mk.reshape(nbp, N_MEM, X_HEADS, X_DH))
        outs['mvp'].append(mv.reshape(nbp, N_MEM, X_HEADS, X_DH))
        outs['cp'].append(cst)
        outs['hrp'].append(hr.reshape(nbp, SSM_GROUPS, SSM_P))
        outs['hip'].append(hi.reshape(nbp, SSM_GROUPS, SSM_P))

        bcx, q_bf, k, k_bf, v, v_bf, u = _in_proj(xs, g_mix, w_in_bf, tabs_s, ms, 1)
        bcx_t = bcx.reshape(nbs, steps, COL_Q).transpose(1, 0, 2)
        st_t = state_conv[l].transpose(1, 0, 2)
        ya_t, nst_t = _conv_sample(bcx_t, st_t, conv_w[l])
        ya = ya_t.transpose(1, 0, 2).reshape(ms, CONV_CH)
        q3 = q_bf.reshape(nbs, steps, DIFF_HEADS, 2, DIFF_DH)
        sel = (jnp.arange(2)[:, None, None, None] == jnp.arange(2)[None, None, None, :]) & \
              (jnp.arange(DIFF_HEADS)[None, :, None, None] == jnp.arange(DIFF_HEADS)[None, None, :, None])
        qbd = jnp.where(sel[None, :, :, None, :, :, None],
                        q3[:, None, None, :, :, :, :], jnp.zeros((), BF16))
        qbd = qbd.reshape(nbs, 2 * DIFF_HEADS * steps, DIFF_WIDTH)
        pad8 = lambda a: jnp.pad(a.reshape(nbs, steps, DIFF_WIDTH), ((0, 0), (0, 8 - steps), (0, 0)))
        yb4 = _attn_sample(page_table, lams, sub, qbd, pad8(k_bf), pad8(v_bf), ck_t, cv_r, l, lam_init)
        yb = yb4.reshape(nbs, DIFF_HEADS, steps, HEAD_W).transpose(0, 2, 1, 3).reshape(ms, DIFF_WIDTH)
        u_t = u.reshape(nbs, steps, SSM_CH).transpose(1, 0, 2)
        yc_t, hr, hi = _s5(u_t, state_ssm_re[l].reshape(nbs, SSM_STATE),
                           state_ssm_im[l].reshape(nbs, SSM_STATE), s5p, steps)
        yc = yc_t.transpose(1, 0, 2).reshape(ms, SSM_CH)
        xs = _residual_proj(xs, [ya, yb, yc], w_o_bf, ms)
        _, xq_bf = _norm_proj(xs, row1(norm_x[l]), [w_xq_bf], ms, scale=1.0 / math.sqrt(X_DH))
        o = _xattn_sample(xq_bf, cache_mem_k[l].reshape(nbs, N_MEM, D_MODEL),
                          cache_mem_v[l].reshape(nbs, N_MEM, D_MODEL), steps, 4)
        xs = _residual_proj(xs, [o], w_xo_bf, ms)
        xs = _mlp(xs, row1(norm_mlp[l]), w_up_bf, w_down_bf, ms, 512)
        outs['ks'].append(k.reshape(nbs, steps, DIFF_HEADS, 2, DIFF_DH))
        outs['vs'].append(v.reshape(nbs, steps, DIFF_HEADS, HEAD_W))
        outs['cs'].append(nst_t.transpose(1, 0, 2))
        outs['hrs'].append(hr.reshape(nbs, SSM_GROUPS, SSM_P))
        outs['his'].append(hi.reshape(nbs, SSM_GROUPS, SSM_P))

    g_fin = row1(norm_final)
    y_prompt = _final_norm(xp, g_fin, 1024).reshape(nbp, seq, D_MODEL)
    y_sample = _final_norm(xs, g_fin, ms).reshape(nbs, steps, D_MODEL)
    st = lambda key: jnp.stack(outs[key])
    return (y_prompt, y_sample, st('kp'), st('vp'), st('mkp'), st('mvp'), st('cp'), st('hrp'), st('hip'),
            st('ks'), st('vs'), st('cs'), st('hrs'), st('his'))
```
